```python
import math
import jax, jax.numpy as jnp
from jax import lax
import numpy as np

D_MODEL = 1024
BATCH = 16
SEQ = 2048
DEPTH = 1
DEC_BATCH = 128
DEC_SEQ = 8
PAST_LEN = 16384
PAGE_SIZE = 128

SSM_EXPAND = 2
D_INNER = SSM_EXPAND * D_MODEL
SSM_HEAD_DIM = 64
SSM_HEADS = D_INNER // SSM_HEAD_DIM
SSM_GROUPS = 8
HEADS_PER_GROUP = SSM_HEADS // SSM_GROUPS
D_STATE = 128
CONV_K = 4
CONV_DIM = D_INNER + 2 * SSM_GROUPS * D_STATE
CHUNK = 128
N_Q_HEADS = 16
N_KV_HEADS = 4
HEAD_DIM = 64
Q_PER_KV = N_Q_HEADS // N_KV_HEADS
WINDOW = 128
ATTN_BLOCK = 128
ROT_DIM = HEAD_DIM // 4
ROPE_THETA = 500000.0
D_FF = 4 * D_MODEL
EPS = 1e-6
PROJ_SIZES = (D_INNER, CONV_DIM, SSM_HEADS, N_Q_HEADS * HEAD_DIM, N_KV_HEADS * HEAD_DIM, N_KV_HEADS * HEAD_DIM, D_MODEL, D_MODEL)
PROJ_DIM = D_INNER + CONV_DIM + SSM_HEADS + (N_Q_HEADS + 2 * N_KV_HEADS) * HEAD_DIM + 2 * D_MODEL

kernel_name = "hybrid_ssd_swa_sink_gated_decoder_step"


def _split_offsets():
    offs, acc = [], 0
    for s in PROJ_SIZES[:-1]:
        acc += s
        offs.append(acc)
    return offs


def _rmsnorm(x, w):
    xf = x.astype(jnp.float32)
    y = xf * lax.rsqrt(jnp.mean(xf * xf, axis=-1, keepdims=True) + EPS)
    return (y * w.astype(jnp.float32)).astype(x.dtype)


def _causal_conv(xbc, conv_state, w, b):
    xp = jnp.concatenate([conv_state.astype(xbc.dtype), xbc], axis=1)
    y = lax.conv_general_dilated(xp, w.astype(xp.dtype)[:, None, :], window_strides=(1,), padding='VALID',
                                 dimension_numbers=('NWC', 'WIO', 'NWC'), feature_group_count=CONV_DIM)
    return jax.nn.silu(y + b.astype(y.dtype)), xp[:, -(CONV_K - 1):]


def _ssd_scan(x, dt, A, B, C, h0):
    f32 = jnp.float32
    bsz, L = x.shape[0], x.shape[1]
    Q = CHUNK if L % CHUNK == 0 else L
    nc = L // Q
    G, Hg, P, N = SSM_GROUPS, HEADS_PER_GROUP, SSM_HEAD_DIM, D_STATE
    xg = jnp.moveaxis(x.astype(f32).reshape(bsz, nc, Q, G, Hg, P), 1, 0)
    dtg = jnp.moveaxis(dt.astype(f32).reshape(bsz, nc, Q, G, Hg), 1, 0)
    Bg = jnp.moveaxis(B.astype(f32).reshape(bsz, nc, Q, G, N), 1, 0)
    Cg = jnp.moveaxis(C.astype(f32).reshape(bsz, nc, Q, G, N), 1, 0)
    Ag = A.astype(f32).reshape(G, Hg)
    causal = jnp.tril(jnp.ones((Q, Q), dtype=bool))[None, :, :, None, None]

    def step(h, inp):
        xc, dtc, Bc, Cc = inp
        acum = jnp.cumsum(dtc * Ag, axis=1)
        seg = acum[:, :, None] - acum[:, None, :]
        decay = jnp.exp(jnp.where(causal, seg, -jnp.inf))
        cb = jnp.einsum('bign,bjgn->bijg', Cc, Bc)
        xdt = xc * dtc[..., None]
        y = jnp.einsum('bijgh,bjghp->bighp', cb[..., None] * decay, xdt)
        y = y + jnp.einsum('bign,bghpn->bighp', Cc, h) * jnp.exp(acum)[..., None]
        last = acum[:, -1]
        w_end = jnp.exp(last[:, None] - acum)
        h = h * jnp.exp(last)[..., None, None] + jnp.einsum('bjgn,bjghp->bghpn', Bc, xdt * w_end[..., None])
        return h, y

    h0g = h0.astype(f32).reshape(bsz, G, Hg, P, N)
    hT, ys = lax.scan(step, h0g, (xg, dtg, Bg, Cg))
    y = jnp.moveaxis(ys, 0, 1).reshape(bsz, L, SSM_HEADS, P)
    return y, hT.reshape(bsz, SSM_HEADS, P, N)


def _ssd_branch(z, xbc, dt_raw, conv_state, ssm_state, conv_w, conv_b, dt_bias, a_log, d_skip, ssm_norm):
    bsz, L = z.shape[0], z.shape[1]
    xbc, new_conv = _causal_conv(xbc, conv_state, conv_w, conv_b)
    xs, Bm, Cm = jnp.split(xbc, [D_INNER, D_INNER + SSM_GROUPS * D_STATE], axis=-1)
    xs = xs.reshape(bsz, L, SSM_HEADS, SSM_HEAD_DIM)
    Bm = Bm.reshape(bsz, L, SSM_GROUPS, D_STATE)
    Cm = Cm.reshape(bsz, L, SSM_GROUPS, D_STATE)
    dt = jax.nn.softplus(dt_raw.astype(jnp.float32) + dt_bias.astype(jnp.float32))
    A = -jnp.exp(a_log.astype(jnp.float32))
    y, new_ssm = _ssd_scan(xs, dt, A, Bm, Cm, ssm_state)
    y = y + xs.astype(jnp.float32) * d_skip.astype(jnp.float32)[:, None]
    y = y.reshape(bsz, L, D_INNER) * jax.nn.silu(z.astype(jnp.float32))
    yg = y.reshape(bsz, L, SSM_GROUPS, D_INNER // SSM_GROUPS)
    yg = yg * lax.rsqrt(jnp.mean(yg * yg, axis=-1, keepdims=True) + EPS)
    y = yg.reshape(bsz, L, D_INNER) * ssm_norm.astype(jnp.float32)
    return y.astype(z.dtype), new_conv, new_ssm


def _rope(x, pos):
    half = ROT_DIM // 2
    inv = ROPE_THETA ** (-jnp.arange(half, dtype=jnp.float32) * 2.0 / ROT_DIM)
    ang = pos.astype(jnp.float32)[:, None] * inv[None, :]
    cos = jnp.cos(ang)[None, :, None, :]
    sin = jnp.sin(ang)[None, :, None, :]
    xf = x.astype(jnp.float32)
    x1, x2 = xf[..., :half], xf[..., half:ROT_DIM]
    out = jnp.concatenate([x1 * cos - x2 * sin, x2 * cos + x1 * sin, xf[..., ROT_DIM:]], axis=-1)
    return out.astype(x.dtype)


def _sink_attention(q, k, v, q_pos, k_pos, sinks):
    s = jnp.einsum('bnqkrd,bnckd->bnkrqc', q.astype(jnp.float32), k.astype(jnp.float32)) * (HEAD_DIM ** -0.5)
    diff = q_pos[:, :, None] - k_pos[:, None, :]
    ok = (diff >= 0) & (diff <= WINDOW) & (k_pos[:, None, :] >= 0)
    s = jnp.where(ok[None, :, None, None], s, -jnp.inf)
    sink = sinks.astype(jnp.float32).reshape(N_KV_HEADS, Q_PER_KV)[None, None, :, :, None, None]
    m = jnp.maximum(jnp.max(s, axis=-1, keepdims=True), sink)
    p = jnp.exp(s - m)
    denom = jnp.sum(p, axis=-1, keepdims=True) + jnp.exp(sink - m)
    return jnp.einsum('bnkrqc,bnckd->bnqkrd', p / denom, v.astype(jnp.float32))


def _attn_prompt(q, k, v, sinks):
    bsz, L = q.shape[0], q.shape[1]
    nb = L // ATTN_BLOCK
    qb = q.reshape(bsz, nb, ATTN_BLOCK, N_KV_HEADS, Q_PER_KV, HEAD_DIM)
    kb = k.reshape(bsz, nb, ATTN_BLOCK, N_KV_HEADS, HEAD_DIM)
    vb = v.reshape(bsz, nb, ATTN_BLOCK, N_KV_HEADS, HEAD_DIM)
    padw = ((0, 0), (1, 0), (0, 0), (0, 0), (0, 0))
    kk = jnp.concatenate([jnp.pad(kb[:, :-1], padw), kb], axis=2)
    vv = jnp.concatenate([jnp.pad(vb[:, :-1], padw), vb], axis=2)
    start = jnp.arange(nb, dtype=jnp.int32)[:, None] * ATTN_BLOCK
    q_pos = start + jnp.arange(ATTN_BLOCK, dtype=jnp.int32)[None, :]
    k_pos = start - ATTN_BLOCK + jnp.arange(2 * ATTN_BLOCK, dtype=jnp.int32)[None, :]
    o = _sink_attention(qb, kk, vv, q_pos, k_pos, sinks)
    return o.reshape(bsz, L, N_Q_HEADS * HEAD_DIM)


def _attn_sample(q, k, v, k_cache, v_cache, sinks):
    bsz, L = q.shape[0], q.shape[1]
    kk = jnp.concatenate([k_cache.astype(k.dtype), k], axis=1)
    vv = jnp.concatenate([v_cache.astype(v.dtype), v], axis=1)
    q_pos = (PAST_LEN + jnp.arange(L, dtype=jnp.int32))[None, :]
    k_pos = (PAST_LEN - WINDOW + jnp.arange(WINDOW + L, dtype=jnp.int32))[None, :]
    qb = q.reshape(bsz, 1, L, N_KV_HEADS, Q_PER_KV, HEAD_DIM)
    o = _sink_attention(qb, kk[:, None], vv[:, None], q_pos, k_pos, sinks)
    return o.reshape(bsz, L, N_Q_HEADS * HEAD_DIM), kk[:, -WINDOW:], vv[:, -WINDOW:]


def _block(x, pos, conv_state, ssm_state, k_cache, v_cache, norm_mix, w_in, conv_w, conv_b, dt_bias, a_log,
           d_skip, ssm_norm, sinks, w_ssm_br, w_attn_br, w_out, norm_mlp, w_up, w_down):
    bsz, L = x.shape[0], x.shape[1]
    if conv_state is None:
        conv_state = jnp.zeros((bsz, CONV_K - 1, CONV_DIM), x.dtype)
        ssm_state = jnp.zeros((bsz, SSM_HEADS, SSM_HEAD_DIM, D_STATE), jnp.float32)
    u = _rmsnorm(x, norm_mix)
    proj = u @ w_in
    z, xbc, dt_raw, q, k, v, g_ssm, g_attn = jnp.split(proj, _split_offsets(), axis=-1)
    y_ssm, new_conv, new_ssm = _ssd_branch(z, xbc, dt_raw, conv_state, ssm_state, conv_w, conv_b,
                                           dt_bias, a_log, d_skip, ssm_norm)
    q = _rope(q.reshape(bsz, L, N_Q_HEADS, HEAD_DIM), pos)
    k = _rope(k.reshape(bsz, L, N_KV_HEADS, HEAD_DIM), pos)
    v = v.reshape(bsz, L, N_KV_HEADS, HEAD_DIM)
    if k_cache is None:
        y_att = _attn_prompt(q, k, v, sinks)
        new_k, new_v = k[:, -WINDOW:], v[:, -WINDOW:]
    else:
        y_att, new_k, new_v = _attn_sample(q, k, v, k_cache, v_cache, sinks)
    y_att = y_att.astype(x.dtype)
    merged = jax.nn.sigmoid(g_ssm) * (y_ssm @ w_ssm_br) + jax.nn.sigmoid(g_attn) * (y_att @ w_attn_br)
    x = x + merged @ w_out
    h = _rmsnorm(x, norm_mlp) @ w_up
    x = x + jnp.square(jax.nn.relu(h)) @ w_down
    return x, new_conv, new_ssm.astype(jnp.float32), new_k, new_v


def setup_inputs(seed: int = 0) -> dict:
    key = jax.random.key(seed)
    ks = jax.random.split(key, 24)
    f32 = jnp.float32
    nrm = lambda k, shape, s: jax.random.normal(k, shape, f32) * s
    dt0 = jnp.exp(jax.random.uniform(ks[7], (DEPTH, SSM_HEADS), f32, math.log(1e-3), math.log(1e-1)))
    return {
        "x_prompt": nrm(ks[0], (BATCH, SEQ, D_MODEL), 1.0),
        "x_sample": nrm(ks[1], (DEC_BATCH, DEC_SEQ, D_MODEL), 1.0),
        "state_ssm": nrm(ks[2], (DEPTH, DEC_BATCH, SSM_HEADS, SSM_HEAD_DIM, D_STATE), 0.3),
        "state_conv": nrm(ks[3], (DEPTH, DEC_BATCH, CONV_K - 1, CONV_DIM), 1.0),
        "cache_k_win": nrm(ks[4], (DEPTH, DEC_BATCH, WINDOW, N_KV_HEADS, HEAD_DIM), 1.0),
        "cache_v_win": nrm(ks[5], (DEPTH, DEC_BATCH, WINDOW, N_KV_HEADS, HEAD_DIM), 1.0),
        "norm_mix": 1.0 + nrm(ks[6], (DEPTH, D_MODEL), 0.01),
        "w_in": nrm(ks[8], (DEPTH, D_MODEL, PROJ_DIM), D_MODEL ** -0.5),
        "conv_w": nrm(ks[9], (DEPTH, CONV_K, CONV_DIM), CONV_K ** -0.5),
        "conv_b": nrm(ks[10], (DEPTH, CONV_DIM), 0.01),
        "dt_bias": dt0 + jnp.log(-jnp.expm1(-dt0)),
        "a_log": jnp.log(jax.random.uniform(ks[11], (DEPTH, SSM_HEADS), f32, 1.0, 16.0)),
        "d_skip": 1.0 + nrm(ks[12], (DEPTH, SSM_HEADS), 0.1),
        "ssm_norm": 1.0 + nrm(ks[13], (DEPTH, D_INNER), 0.01),
        "sinks": nrm(ks[14], (DEPTH, N_Q_HEADS), 0.5),
        "w_ssm_br": nrm(ks[15], (DEPTH, D_INNER, D_MODEL), D_INNER ** -0.5),
        "w_attn_br": nrm(ks[16], (DEPTH, N_Q_HEADS * HEAD_DIM, D_MODEL), (N_Q_HEADS * HEAD_DIM) ** -0.5),
        "w_out": nrm(ks[17], (DEPTH, D_MODEL, D_MODEL), D_MODEL ** -0.5),
        "norm_mlp": 1.0 + nrm(ks[18], (DEPTH, D_MODEL), 0.01),
        "w_up": nrm(ks[19], (DEPTH, D_MODEL, D_FF), D_MODEL ** -0.5),
        "w_down": nrm(ks[20], (DEPTH, D_FF, D_MODEL), D_FF ** -0.5),
        "norm_final": 1.0 + nrm(ks[21], (D_MODEL,), 0.01),
    }


def reference(x_prompt, x_sample, state_ssm, state_conv, cache_k_win, cache_v_win, norm_mix, w_in, conv_w,
              conv_b, dt_bias, a_log, d_skip, ssm_norm, sinks, w_ssm_br, w_attn_br, w_out, norm_mlp, w_up,
              w_down, norm_final):
    pos_p = jnp.arange(x_prompt.shape[1], dtype=jnp.int32)
    pos_s = PAST_LEN + jnp.arange(x_sample.shape[1], dtype=jnp.int32)
    hp, hs = x_prompt, x_sample
    ssm_p, ssm_s, conv_p, conv_s, k_p, k_s, v_p, v_s = [], [], [], [], [], [], [], []
    for l in range(DEPTH):
        lw = (norm_mix[l], w_in[l], conv_w[l], conv_b[l], dt_bias[l], a_log[l], d_skip[l], ssm_norm[l],
              sinks[l], w_ssm_br[l], w_attn_br[l], w_out[l], norm_mlp[l], w_up[l], w_down[l])
        hp, c1, s1, k1, v1 = _block(hp, pos_p, None, None, None, None, *lw)
        hs, c2, s2, k2, v2 = _block(hs, pos_s, state_conv[l], state_ssm[l], cache_k_win[l], cache_v_win[l], *lw)
        ssm_p.append(s1); ssm_s.append(s2); conv_p.append(c1); conv_s.append(c2)
        k_p.append(k1); k_s.append(k2); v_p.append(v1); v_s.append(v2)
    y_prompt = _rmsnorm(hp, norm_final)
    y_sample = _rmsnorm(hs, norm_final)
    return (y_prompt, y_sample, jnp.stack(ssm_p), jnp.stack(ssm_s), jnp.stack(conv_p), jnp.stack(conv_s),
            jnp.stack(k_p), jnp.stack(k_s), jnp.stack(v_p), jnp.stack(v_s))
```

```python
import functools
import math

import jax
import jax.numpy as jnp
from jax import lax
from jax.experimental import pallas as pl
from jax.experimental.pallas import tpu as pltpu

D_MODEL = 1024
D_INNER = 2048
SSM_HEAD_DIM = 64
SSM_HEADS = 32
SSM_GROUPS = 8
HEADS_PER_GROUP = 4
GROUP_WIDTH = HEADS_PER_GROUP * SSM_HEAD_DIM
D_STATE = 128
CONV_K = 4
CONV_DIM = 4096
CHUNK = 128
N_Q_HEADS = 16
N_KV_HEADS = 4
HEAD_DIM = 64
Q_PER_KV = 4
WINDOW = 128
ATTN_BLOCK = 128
ROT_DIM = 16
ROPE_THETA = 500000.0
D_FF = 4096
EPS = 1e-6
PAST_LEN = 16384
Q_WIDTH = N_Q_HEADS * HEAD_DIM
KV_WIDTH = N_KV_HEADS * HEAD_DIM
QKV_WIDTH = Q_WIDTH + 2 * KV_WIDTH

LANES = 128
SUBLANES = 8
VMEM_LIMIT = 56 * 1024 * 1024

F32 = jnp.float32
BF16 = jnp.bfloat16


def _resident(shape):
    zeros = (0,) * len(shape)
    return pl.BlockSpec(shape, lambda *_: zeros, pipeline_mode=pl.Buffered(1))


def _split3(a):
    hi = a.astype(BF16)
    r = a - hi.astype(F32)
    mid = r.astype(BF16)
    lo = (r - mid.astype(F32)).astype(BF16)
    return hi, mid, lo


def _dot(a, b):
    return jnp.dot(a, b, preferred_element_type=F32)


def _dot_nt(a, b):
    return lax.dot_general(a, b, (((1,), (1,)), ((), ())), preferred_element_type=F32)


def _silu(x):
    return x * (1.0 / (1.0 + jnp.exp(-x)))


def _sigmoid(x):
    return 1.0 / (1.0 + jnp.exp(-x))


def _rope_tile(x, cos, s1, s2):
    return x * cos + pltpu.roll(x, LANES - ROT_DIM // 2, 1) * s1 + pltpu.roll(x, ROT_DIM // 2, 1) * s2


def _proj_kernel(x_ref, nw_ref, wz_ref, wx_ref, wqkv_ref, wg_ref, wdt_ref, cos_ref, s1_ref, s2_ref,
                 z_ref, xbc_ref, qkv_ref, g_ref, dt_ref, *tail_refs, tiles_per_seq):
    tm = x_ref.shape[0]
    x = x_ref[...]
    u = (x * lax.rsqrt(jnp.mean(x * x, axis=-1, keepdims=True) + EPS) * nw_ref[...]).astype(BF16)
    emit_tails = len(tail_refs) > 0
    if emit_tails:
        xtail_ref, kvtail_ref = tail_refs
        is_last = (pl.program_id(0) % tiles_per_seq) == tiles_per_seq - 1

    step = 512
    for c in range(0, D_INNER, step):
        z_ref[:, c:c + step] = _dot(u, wz_ref[:, c:c + step]).astype(z_ref.dtype)
    for c in range(0, CONV_DIM, step):
        r = _dot(u, wx_ref[:, c:c + step])
        xbc_ref[:, c:c + step] = r.astype(xbc_ref.dtype)
        if emit_tails:
            @pl.when(is_last)
            def _(r=r, c=c):
                xtail_ref[0, :, c:c + step] = r[tm - SUBLANES:, :]
    for c in range(0, 2 * D_MODEL, step):
        g_ref[:, c:c + step] = _dot(u, wg_ref[:, c:c + step]).astype(g_ref.dtype)
    dt_ref[...] = _dot(u, wdt_ref[...])

    cos, s1, s2 = cos_ref[...], s1_ref[...], s2_ref[...]
    for c in range(0, Q_WIDTH + KV_WIDTH, LANES):
        r = _rope_tile(_dot(u, wqkv_ref[:, c:c + LANES]), cos, s1, s2)
        qkv_ref[:, c:c + LANES] = r.astype(qkv_ref.dtype)
        if emit_tails and c >= Q_WIDTH:
            @pl.when(is_last)
            def _(r=r, c=c):
                kvtail_ref[0, :, c - Q_WIDTH:c - Q_WIDTH + LANES] = r[tm - WINDOW:, :]
    for c in range(Q_WIDTH + KV_WIDTH, QKV_WIDTH, LANES):
        r = _dot(u, wqkv_ref[:, c:c + LANES])
        qkv_ref[:, c:c + LANES] = r.astype(qkv_ref.dtype)
        if emit_tails:
            @pl.when(is_last)
            def _(r=r, c=c):
                kvtail_ref[0, :, c - Q_WIDTH:c - Q_WIDTH + LANES] = r[tm - WINDOW:, :]


def _proj(x2d, nw, wts, rope_tabs, *, tm, seq_len, out_dtype, emit_tails):
    m = x2d.shape[0]
    n_tiles = m // tm
    tab_tiles = rope_tabs[0].shape[0] // tm
    tiles_per_seq = max(seq_len // tm, 1)
    wz, wx, wqkv, wg, wdt = wts
    row = lambda w: pl.BlockSpec((tm, w), lambda i: (i, 0))
    tab = pl.BlockSpec((tm, LANES), lambda i: (i % tab_tiles, 0))
    in_specs = [row(D_MODEL), _resident((1, D_MODEL)), _resident(wz.shape), _resident(wx.shape),
                _resident(wqkv.shape), _resident(wg.shape), _resident(wdt.shape), tab, tab, tab]
    out_shape = [jax.ShapeDtypeStruct((m, D_INNER), out_dtype),
                 jax.ShapeDtypeStruct((m, CONV_DIM), out_dtype),
                 jax.ShapeDtypeStruct((m, QKV_WIDTH), out_dtype),
                 jax.ShapeDtypeStruct((m, 2 * D_MODEL), out_dtype),
                 jax.ShapeDtypeStruct((m, LANES), F32)]
    out_specs = [row(D_INNER), row(CONV_DIM), row(QKV_WIDTH), row(2 * D_MODEL), row(LANES)]
    if emit_tails:
        n_seq = m // seq_len
        out_shape += [jax.ShapeDtypeStruct((n_seq, SUBLANES, CONV_DIM), F32),
                      jax.ShapeDtypeStruct((n_seq, WINDOW, 2 * KV_WIDTH), F32)]
        out_specs += [pl.BlockSpec((1, SUBLANES, CONV_DIM), lambda i: (i // tiles_per_seq, 0, 0)),
                      pl.BlockSpec((1, WINDOW, 2 * KV_WIDTH), lambda i: (i // tiles_per_seq, 0, 0))]
    return pl.pallas_call(
        functools.partial(_proj_kernel, tiles_per_seq=tiles_per_seq),
        grid=(n_tiles,), in_specs=in_specs, out_specs=out_specs, out_shape=out_shape,
        compiler_params=pltpu.CompilerParams(dimension_semantics=("arbitrary",),
                                             vmem_limit_bytes=VMEM_LIMIT),
        name="proj",
    )(x2d, nw, wz, wx, wqkv, wg, wdt, *rope_tabs)


CONV_COLS = 512


def _softplus(x):
    return jnp.maximum(x, 0.0) + jnp.log1p(jnp.exp(-jnp.abs(x)))


def _ssd_decay_terms(dt_ref, dtb_ref, alog_ref, tri_ref, e_ref, ea_ref):
    tri = tri_ref[...]
    allowed = tri > 0
    dtv = _softplus(dt_ref[...] + dtb_ref[...])
    dta = dtv * (-jnp.exp(alog_ref[...]))
    acum = sum(_dot(tri, p) for p in _split3(dta))
    e = e_ref[...]
    ea_ref[...] = sum(_dot(p, e) for p in _split3(jnp.exp(acum)))
    return dtv, acum, acum.T, dtv.T, allowed


def _ssd_groups(xc_ref, y_ref, acum, acum_t, dt_t, allowed):
    t = xc_ref.shape[0]
    lane_head = lax.broadcasted_iota(jnp.int32, (t, GROUP_WIDTH), 1) // SSM_HEAD_DIM
    for g in range(SSM_GROUPS):
        bg = xc_ref[:, D_INNER + g * D_STATE:D_INNER + (g + 1) * D_STATE].astype(BF16)
        cg = xc_ref[:, D_INNER + SSM_GROUPS * D_STATE + g * D_STATE:
                    D_INNER + SSM_GROUPS * D_STATE + (g + 1) * D_STATE].astype(BF16)
        cb = _dot_nt(cg, bg)
        xg = xc_ref[:, g * GROUP_WIDTH:(g + 1) * GROUP_WIDTH]
        ms, bd = [], []
        for k in range(HEADS_PER_GROUP):
            h = g * HEADS_PER_GROUP + k
            seg = acum[:, h:h + 1] - acum_t[h:h + 1, :]
            dec = jnp.exp(jnp.where(allowed, seg, -jnp.inf))
            ms.append((cb * dec * dt_t[h:h + 1, :]).astype(BF16))
            bd.append(jnp.where(lane_head == k, xg, 0.0).astype(BF16))
        y_ref[:, g * GROUP_WIDTH:(g + 1) * GROUP_WIDTH] = _dot(jnp.concatenate(ms, axis=1),
                                                               jnp.concatenate(bd, axis=0))


def _ssd_finish(y, xs, z, dskip, nrm):
    y = (y + xs * dskip) * _silu(z)
    return y * lax.rsqrt(jnp.mean(y * y, axis=-1, keepdims=True) + EPS) * nrm


def _ssd_prompt_kernel(z_ref, xbc_ref, dt_ref, cw_ref, cb_ref, dtb_ref, alog_ref, dskip_ref, nrm_ref,
                       tri_ref, e_ref, y_out_ref, st_out_ref,
                       xc_ref, y_ref, ea_ref, tail_ref, ht_ref):
    c = pl.program_id(1)
    t = xbc_ref.shape[0]

    @pl.when(c == 0)
    def _():
        tail_ref[...] = jnp.zeros_like(tail_ref)
        ht_ref[...] = jnp.zeros_like(ht_ref)

    row8 = lax.broadcasted_iota(jnp.int32, (SUBLANES, CONV_COLS), 0)

    def conv_cols(j, carry):
        off = pl.multiple_of(j * CONV_COLS, CONV_COLS)
        xr = xbc_ref[:, pl.ds(off, CONV_COLS)].astype(F32)
        tl = tail_ref[:, pl.ds(off, CONV_COLS)]
        cw = cw_ref[:, pl.ds(off, CONV_COLS)]
        acc = xr * cw[CONV_K - 1:CONV_K, :] + cb_ref[:, pl.ds(off, CONV_COLS)]
        for k in range(1, CONV_K):
            rolled = pltpu.roll(xr, k, 0)
            top = jnp.where(row8 < k, pltpu.roll(tl, k, 0), rolled[:SUBLANES])
            shifted = jnp.concatenate([top, rolled[SUBLANES:]], axis=0)
            acc = acc + shifted * cw[CONV_K - 1 - k:CONV_K - k, :]
        tail_ref[:, pl.ds(off, CONV_COLS)] = xr[t - SUBLANES:]
        xc_ref[:, pl.ds(off, CONV_COLS)] = _silu(acc)
        return carry

    lax.fori_loop(0, CONV_DIM // CONV_COLS, conv_cols, 0)

    dtv, acum, acum_t, dt_t, allowed = _ssd_decay_terms(dt_ref, dtb_ref, alog_ref, tri_ref, e_ref, ea_ref)
    _ssd_groups(xc_ref, y_ref, acum, acum_t, dt_t, allowed)

    last = acum[t - 1:t, :]
    send = dtv * jnp.exp(last - acum)
    e = e_ref[...]
    send_x = sum(_dot(p, e) for p in _split3(send))

    for g in range(SSM_GROUPS):
        cols = slice(g * GROUP_WIDTH, (g + 1) * GROUP_WIDTH)
        bg = xc_ref[:, D_INNER + g * D_STATE:D_INNER + (g + 1) * D_STATE]
        cg = xc_ref[:, D_INNER + SSM_GROUPS * D_STATE + g * D_STATE:
                    D_INNER + SSM_GROUPS * D_STATE + (g + 1) * D_STATE].astype(BF16)
        xs = xc_ref[:, cols]
        ea = ea_ref[:, cols]
        ht = ht_ref[g]
        y = y_ref[:, cols] + _dot(cg, ht.astype(BF16)) * ea
        xw = (xs * send_x[:, cols]).astype(BF16)
        ht_ref[g] = ht * ea[t - 1:t, :] + _dot(bg.T.astype(BF16), xw)
        y_out_ref[:, cols] = _ssd_finish(y, xs, z_ref[:, cols].astype(F32), dskip_ref[:, cols],
                                         nrm_ref[:, cols]).astype(y_out_ref.dtype)

    @pl.when(c == pl.num_programs(1) - 1)
    def _():
        for g in range(SSM_GROUPS):
            st_out_ref[0, g * GROUP_WIDTH:(g + 1) * GROUP_WIDTH, :] = ht_ref[g].T


def _ssd_prompt(z, xbc, dt, params, tri, e, *, n_seq, n_chunks):
    m = z.shape[0]
    cw, cb, dtb, alog, dskip, nrm = params
    row = lambda w: pl.BlockSpec((CHUNK, w), lambda b, c: (b * n_chunks + c, 0))
    in_specs = [row(D_INNER), row(CONV_DIM), row(LANES)] + [_resident(p.shape) for p in params] + \
               [_resident(tri.shape), _resident(e.shape)]
    return pl.pallas_call(
        _ssd_prompt_kernel,
        grid=(n_seq, n_chunks), in_specs=in_specs,
        out_specs=[row(D_INNER), pl.BlockSpec((1, D_INNER, D_STATE), lambda b, c: (b, 0, 0))],
        out_shape=[jax.ShapeDtypeStruct((m, D_INNER), BF16),
                   jax.ShapeDtypeStruct((n_seq, D_INNER, D_STATE), F32)],
        scratch_shapes=[pltpu.VMEM((CHUNK, CONV_DIM), F32), pltpu.VMEM((CHUNK, D_INNER), F32),
                        pltpu.VMEM((CHUNK, D_INNER), F32),
                        pltpu.VMEM((SUBLANES, CONV_DIM), F32),
                        pltpu.VMEM((SSM_GROUPS, D_STATE, GROUP_WIDTH), F32)],
        compiler_params=pltpu.CompilerParams(dimension_semantics=("arbitrary", "arbitrary"),
                                             vmem_limit_bytes=VMEM_LIMIT),
        name="ssd_prompt",
    )(z, xbc, dt, cw, cb, dtb, alog, dskip, nrm, tri, e)


def _ssd_sample_kernel(z_ref, xbc_ref, dt_ref, cst_ref, st_ref, cw_ref, cb_ref, dtb_ref, alog_ref,
                       dskip_ref, nrm_ref, tri_ref, e_ref, y_out_ref, st_out_ref,
                       xc_ref, y_ref, ea_ref, xw_ref, *, seq_len):
    j = pl.program_id(1)
    t = xbc_ref.shape[0]

    @pl.when(j == 0)
    def _():
        tok = lax.broadcasted_iota(jnp.int32, (t, CONV_COLS), 0) % seq_len

        def conv_cols(i, carry):
            off = pl.multiple_of(i * CONV_COLS, CONV_COLS)
            xr = xbc_ref[:, pl.ds(off, CONV_COLS)].astype(F32)
            cs = cst_ref[:, pl.ds(off, CONV_COLS)]
            cw = cw_ref[:, pl.ds(off, CONV_COLS)]
            acc = xr * cw[CONV_K - 1:CONV_K, :] + cb_ref[:, pl.ds(off, CONV_COLS)]
            for k in range(1, CONV_K):
                shifted = jnp.where(tok < k, pltpu.roll(cs, t - seq_len + k, 0), pltpu.roll(xr, k, 0))
                acc = acc + shifted * cw[CONV_K - 1 - k:CONV_K - k, :]
            xc_ref[:, pl.ds(off, CONV_COLS)] = _silu(acc)
            return carry

        lax.fori_loop(0, CONV_DIM // CONV_COLS, conv_cols, 0)
        dtv, acum, acum_t, dt_t, allowed = _ssd_decay_terms(dt_ref, dtb_ref, alog_ref, tri_ref, e_ref, ea_ref)
        _ssd_groups(xc_ref, y_ref, acum, acum_t, dt_t, allowed)
        tokr = lax.broadcasted_iota(jnp.int32, (t, LANES), 0) % seq_len
        last = jnp.where(tokr == seq_len - 1, acum, 0.0)
        for s in range(1, seq_len):
            rolled = pltpu.roll(acum, t - s, 0)
            last = last + jnp.where(tokr == seq_len - 1 - s, rolled, 0.0)
        send = dtv * jnp.exp(last - acum)
        e = e_ref[...]
        send_x = sum(_dot(p, e) for p in _split3(send))
        xw_ref[...] = xc_ref[:, :D_INNER] * send_x

    r0 = pl.multiple_of(j * seq_len, SUBLANES)
    rows = pl.ds(r0, seq_len)
    sub = lax.broadcasted_iota(jnp.int32, (SUBLANES, GROUP_WIDTH), 0)
    zeros_b = jnp.zeros((SUBLANES, D_STATE), F32)
    ones_b = jnp.ones((SUBLANES, D_STATE), F32)
    for g in range(SSM_GROUPS):
        cols = slice(g * GROUP_WIDTH, (g + 1) * GROUP_WIDTH)
        hg = st_ref[0, cols, :]
        bg = xc_ref[rows, D_INNER + g * D_STATE:D_INNER + (g + 1) * D_STATE]
        cg = xc_ref[rows, D_INNER + SSM_GROUPS * D_STATE + g * D_STATE:
                    D_INNER + SSM_GROUPS * D_STATE + (g + 1) * D_STATE]
        ea = ea_ref[rows, cols]
        c16 = jnp.concatenate([cg, jnp.zeros_like(cg)], axis=0).astype(BF16)
        y_ref[rows, cols] = y_ref[rows, cols] + _dot_nt(c16, hg.astype(BF16))[:seq_len] * ea
        v = ea[seq_len - 1:seq_len, :]
        vh, vm, vl = (p.astype(F32) for p in _split3(v))
        bot = jnp.where(sub == 0, vh, jnp.where(sub == 1, vm, jnp.where(sub == 2, vl, 0.0)))
        lhs = jnp.concatenate([xw_ref[rows, cols], bot], axis=0).astype(BF16)
        rhs = jnp.concatenate([jnp.concatenate([bg, zeros_b], axis=1),
                               jnp.concatenate([zeros_b, ones_b], axis=1)], axis=0).astype(BF16)
        out = lax.dot_general(lhs, rhs, (((0,), (0,)), ((), ())), preferred_element_type=F32)
        st_out_ref[0, cols, :] = out[:, D_STATE:] * hg + out[:, :D_STATE]

    @pl.when(j == pl.num_programs(1) - 1)
    def _():
        for g in range(SSM_GROUPS):
            cols = slice(g * GROUP_WIDTH, (g + 1) * GROUP_WIDTH)
            y_out_ref[:, cols] = _ssd_finish(y_ref[:, cols], xc_ref[:, cols], z_ref[:, cols].astype(F32),
                                             dskip_ref[:, cols], nrm_ref[:, cols]).astype(y_out_ref.dtype)


def _ssd_sample(z, xbc, dt, conv_state_rows, state, params, tri, e, *, seq_len, seqs_per_block):
    m = z.shape[0]
    t = seq_len * seqs_per_block
    n_blocks = m // t
    cw, cb, dtb, alog, dskip, nrm = params
    row = lambda w: pl.BlockSpec((t, w), lambda i, j: (i, 0))
    st = pl.BlockSpec((1, D_INNER, D_STATE), lambda i, j: (i * seqs_per_block + j, 0, 0))
    in_specs = [row(D_INNER), row(CONV_DIM), row(LANES), row(CONV_DIM), st] + \
               [_resident(p.shape) for p in params] + [_resident(tri.shape), _resident(e.shape)]
    return pl.pallas_call(
        functools.partial(_ssd_sample_kernel, seq_len=seq_len),
        grid=(n_blocks, seqs_per_block), in_specs=in_specs,
        out_specs=[row(D_INNER), st],
        out_shape=[jax.ShapeDtypeStruct((m, D_INNER), BF16),
                   jax.ShapeDtypeStruct(state.shape, F32)],
        scratch_shapes=[pltpu.VMEM((t, CONV_DIM), F32), pltpu.VMEM((t, D_INNER), F32),
                        pltpu.VMEM((t, D_INNER), F32), pltpu.VMEM((t, D_INNER), F32)],
        compiler_params=pltpu.CompilerParams(dimension_semantics=("arbitrary", "arbitrary"),
                                             vmem_limit_bytes=VMEM_LIMIT),
        name="ssd_sample",
    )(z, xbc, dt, conv_state_rows, state, cw, cb, dtb, alog, dskip, nrm, tri, e)


def _softmax_sink_pv(s_parts, v_parts, sink_col):
    m = sink_col
    for s in s_parts:
        m = jnp.maximum(m, jnp.max(s, axis=-1, keepdims=True))
    den = jnp.exp(sink_col - m)
    o = None
    for s, v in zip(s_parts, v_parts):
        p = jnp.exp(s - m)
        den = den + jnp.sum(p, axis=-1, keepdims=True)
        pv = _dot(p.astype(BF16), v)
        o = pv if o is None else o + pv
    return o * (1.0 / den)


def _sink_column(sinks_ref, kv, rows_per_head):
    rows = lax.broadcasted_iota(jnp.int32, (Q_PER_KV * rows_per_head, 1), 0) // rows_per_head
    col = jnp.zeros((Q_PER_KV * rows_per_head, 1), F32)
    for r in range(Q_PER_KV):
        col = jnp.where(rows == r, sinks_ref[kv * Q_PER_KV + r], col)
    return col


def _attn_prompt_kernel(sinks_ref, q_ref, kp_ref, kc_ref, vp_ref, vc_ref, o_ref):
    n = pl.program_id(1)
    blk = q_ref.shape[0]
    r = lax.broadcasted_iota(jnp.int32, (Q_PER_KV * blk, 2 * blk), 0) % blk
    c = lax.broadcasted_iota(jnp.int32, (Q_PER_KV * blk, 2 * blk), 1)
    ok = (c >= r) & (c <= r + WINDOW) & ((c >= blk) | (n > 0))
    scale = HEAD_DIM ** -0.5
    for kv in range(N_KV_HEADS):
        hs = slice(kv * HEAD_DIM, (kv + 1) * HEAD_DIM)
        kk = jnp.concatenate([kp_ref[:, hs], kc_ref[:, hs]], axis=0)
        vv = jnp.concatenate([vp_ref[:, hs], vc_ref[:, hs]], axis=0)
        qs = jnp.concatenate([q_ref[:, (kv * Q_PER_KV + i) * HEAD_DIM:(kv * Q_PER_KV + i + 1) * HEAD_DIM]
                              for i in range(Q_PER_KV)], axis=0)
        s = jnp.where(ok, _dot_nt(qs, kk) * scale, -jnp.inf)
        o = _softmax_sink_pv([s], [vv], _sink_column(sinks_ref, kv, blk))
        for i in range(Q_PER_KV):
            h = kv * Q_PER_KV + i
            o_ref[:, h * HEAD_DIM:(h + 1) * HEAD_DIM] = o[i * blk:(i + 1) * blk].astype(o_ref.dtype)


def _attn_prompt(qkv, sinks, *, n_seq, n_blocks):
    m = qkv.shape[0]
    kcol, vcol = Q_WIDTH // KV_WIDTH, Q_WIDTH // KV_WIDTH + 1
    cur = lambda b, n: b * n_blocks + n
    prev = lambda b, n: b * n_blocks + jnp.maximum(n - 1, 0)
    in_specs = [pl.BlockSpec(memory_space=pltpu.SMEM),
                pl.BlockSpec((ATTN_BLOCK, Q_WIDTH), lambda b, n: (cur(b, n), 0)),
                pl.BlockSpec((ATTN_BLOCK, KV_WIDTH), lambda b, n: (prev(b, n), kcol)),
                pl.BlockSpec((ATTN_BLOCK, KV_WIDTH), lambda b, n: (cur(b, n), kcol)),
                pl.BlockSpec((ATTN_BLOCK, KV_WIDTH), lambda b, n: (prev(b, n), vcol)),
                pl.BlockSpec((ATTN_BLOCK, KV_WIDTH), lambda b, n: (cur(b, n), vcol))]
    return pl.pallas_call(
        _attn_prompt_kernel, grid=(n_seq, n_blocks), in_specs=in_specs,
        out_specs=pl.BlockSpec((ATTN_BLOCK, Q_WIDTH), lambda b, n: (cur(b, n), 0)),
        out_shape=jax.ShapeDtypeStruct((m, Q_WIDTH), BF16),
        compiler_params=pltpu.CompilerParams(dimension_semantics=("arbitrary", "arbitrary"),
                                             vmem_limit_bytes=VMEM_LIMIT),
        name="attn_prompt",
    )(sinks, qkv, qkv, qkv, qkv, qkv)


def _attn_sample_kernel(sinks_ref, q_ref, kn_ref, vn_ref, ck_ref, cv_ref, o_ref, ko_ref, vo_ref):
    sl = q_ref.shape[0]
    ck, cv = ck_ref[0], cv_ref[0]
    kn, vn = kn_ref[...], vn_ref[...]
    ko_ref[0, :WINDOW - sl] = ck[sl:]
    ko_ref[0, WINDOW - sl:] = kn
    vo_ref[0, :WINDOW - sl] = cv[sl:]
    vo_ref[0, WINDOW - sl:] = vn
    rows = Q_PER_KV * sl
    tq_c = lax.broadcasted_iota(jnp.int32, (rows, WINDOW), 0) % sl
    c_c = lax.broadcasted_iota(jnp.int32, (rows, WINDOW), 1)
    ok_cache = c_c >= tq_c
    tq_n = lax.broadcasted_iota(jnp.int32, (rows, 2 * sl), 0) % sl
    c_n = lax.broadcasted_iota(jnp.int32, (rows, 2 * sl), 1)
    ok_new = c_n <= tq_n
    scale = HEAD_DIM ** -0.5
    pad = jnp.zeros((sl, KV_WIDTH), F32)
    kn16 = jnp.concatenate([kn, pad], axis=0).astype(BF16)
    vn16 = jnp.concatenate([vn, pad], axis=0).astype(BF16)
    ckb, cvb = ck.astype(BF16), cv.astype(BF16)
    for kv in range(N_KV_HEADS):
        hs = slice(kv * HEAD_DIM, (kv + 1) * HEAD_DIM)
        qs = jnp.concatenate([q_ref[:, (kv * Q_PER_KV + i) * HEAD_DIM:(kv * Q_PER_KV + i + 1) * HEAD_DIM]
                              for i in range(Q_PER_KV)], axis=0).astype(BF16)
        s_c = jnp.where(ok_cache, _dot_nt(qs, ckb[:, hs]) * scale, -jnp.inf)
        s_n = jnp.where(ok_new, _dot_nt(qs, kn16[:, hs]) * scale, -jnp.inf)
        o = _softmax_sink_pv([s_c, s_n], [cvb[:, hs], vn16[:, hs]], _sink_column(sinks_ref, kv, sl))
        for i in range(Q_PER_KV):
            h = kv * Q_PER_KV + i
            o_ref[:, h * HEAD_DIM:(h + 1) * HEAD_DIM] = o[i * sl:(i + 1) * sl].astype(o_ref.dtype)


def _attn_sample(qkv, cache_k, cache_v, sinks, *, seq_len):
    m = qkv.shape[0]
    n_seq = m // seq_len
    kcol, vcol = Q_WIDTH // KV_WIDTH, Q_WIDTH // KV_WIDTH + 1
    cache = pl.BlockSpec((1, WINDOW, KV_WIDTH), lambda b: (b, 0, 0))
    in_specs = [pl.BlockSpec(memory_space=pltpu.SMEM),
                pl.BlockSpec((seq_len, Q_WIDTH), lambda b: (b, 0)),
                pl.BlockSpec((seq_len, KV_WIDTH), lambda b: (b, kcol)),
                pl.BlockSpec((seq_len, KV_WIDTH), lambda b: (b, vcol)),
                cache, cache]
    return pl.pallas_call(
        _attn_sample_kernel, grid=(n_seq,), in_specs=in_specs,
        out_specs=[pl.BlockSpec((seq_len, Q_WIDTH), lambda b: (b, 0)), cache, cache],
        out_shape=[jax.ShapeDtypeStruct((m, Q_WIDTH), BF16),
                   jax.ShapeDtypeStruct(cache_k.shape, F32), jax.ShapeDtypeStruct(cache_v.shape, F32)],
        compiler_params=pltpu.CompilerParams(dimension_semantics=("arbitrary",),
                                             vmem_limit_bytes=VMEM_LIMIT),
        name="attn_sample",
    )(sinks, qkv, qkv, qkv, cache_k, cache_v)


FF_COLS = 1024


def _mix_kernel(x_ref, ys_ref, ya_ref, g_ref, wsb_ref, wab_ref, wo_ref, nm_ref, wu_ref, wd_ref, nf_ref, o_ref):
    g = g_ref[...].astype(F32)
    merged = (_sigmoid(g[:, :D_MODEL]) * _dot(ys_ref[...], wsb_ref[...])
              + _sigmoid(g[:, D_MODEL:]) * _dot(ya_ref[...], wab_ref[...]))
    x1 = x_ref[...] + _dot(merged.astype(BF16), wo_ref[...])
    hn = (x1 * lax.rsqrt(jnp.mean(x1 * x1, axis=-1, keepdims=True) + EPS) * nm_ref[...]).astype(BF16)
    x2 = x1
    for c in range(0, D_FF, FF_COLS):
        h = jnp.maximum(_dot(hn, wu_ref[:, c:c + FF_COLS]), 0.0)
        x2 = x2 + _dot((h * h).astype(BF16), wd_ref[c:c + FF_COLS, :])
    o_ref[...] = x2 * lax.rsqrt(jnp.mean(x2 * x2, axis=-1, keepdims=True) + EPS) * nf_ref[...]


def _mix(x2d, ys, ya, g, wts, *, tm):
    m = x2d.shape[0]
    wsb, wab, wo, nm, wu, wd, nf = wts
    row = lambda w: pl.BlockSpec((tm, w), lambda i: (i, 0))
    in_specs = [row(D_MODEL), row(D_INNER), row(Q_WIDTH), row(2 * D_MODEL)] + [_resident(w.shape) for w in wts]
    return pl.pallas_call(
        _mix_kernel, grid=(m // tm,), in_specs=in_specs, out_specs=row(D_MODEL),
        out_shape=jax.ShapeDtypeStruct((m, D_MODEL), F32),
        compiler_params=pltpu.CompilerParams(dimension_semantics=("arbitrary",),
                                             vmem_limit_bytes=VMEM_LIMIT),
        name="mix",
    )(x2d, ys, ya, g, *wts)


def _rope_tables(pos):
    half = ROT_DIM // 2
    inv = ROPE_THETA ** (-jnp.arange(half, dtype=F32) * 2.0 / ROT_DIM)
    ang = pos.astype(F32)[:, None] * inv[None, :]
    cos, sin = jnp.cos(ang), jnp.sin(ang)
    n = pos.shape[0]
    ones = jnp.ones((n, HEAD_DIM - ROT_DIM), F32)
    zeros = jnp.zeros((n, HEAD_DIM - ROT_DIM), F32)
    zh = jnp.zeros((n, half), F32)
    c = jnp.concatenate([cos, cos, ones], axis=1)
    s1 = jnp.concatenate([-sin, zh, zeros], axis=1)
    s2 = jnp.concatenate([zh, sin, zeros], axis=1)
    tile2 = lambda a: jnp.concatenate([a, a], axis=1)
    return tile2(c), tile2(s1), tile2(s2)


def _same_seq_causal(t, seq_len):
    r = jnp.arange(t)[:, None]
    c = jnp.arange(t)[None, :]
    return ((r // seq_len == c // seq_len) & (c <= r)).astype(BF16)


def kernel(x_prompt, x_sample, state_ssm, state_conv, cache_k_win, cache_v_win, norm_mix, w_in, conv_w,
           conv_b, dt_bias, a_log, d_skip, ssm_norm, sinks, w_ssm_br, w_attn_br, w_out, norm_mlp, w_up,
           w_down, norm_final):
    nb, seq, _ = x_prompt.shape
    db, dseq, _ = x_sample.shape
    assert norm_mix.shape[0] == 1, "single-layer step"
    assert seq % CHUNK == 0 and dseq == SUBLANES and dseq < CHUNK

    w = w_in[0]
    o_z, o_x, o_dt, o_qkv, o_g = 0, D_INNER, D_INNER + CONV_DIM, D_INNER + CONV_DIM + SSM_HEADS, \
        D_INNER + CONV_DIM + SSM_HEADS + QKV_WIDTH
    wz = w[:, o_z:o_x].astype(BF16)
    wx = w[:, o_x:o_dt].astype(BF16)
    wdt = jnp.pad(w[:, o_dt:o_qkv], ((0, 0), (0, LANES - SSM_HEADS))).astype(BF16)
    wqkv = w[:, o_qkv:o_g].astype(BF16)
    wg = w[:, o_g:].astype(BF16)
    proj_w = (wz, wx, wqkv, wg, wdt)
    nw = norm_mix[0][None, :]
    pad_heads = lambda a: jnp.pad(a[0], (0, LANES - SSM_HEADS))[None, :]
    ssd_params = (conv_w[0], conv_b[0][None, :], pad_heads(dt_bias), pad_heads(a_log),
                  jnp.repeat(d_skip[0], SSM_HEAD_DIM)[None, :], ssm_norm[0][None, :])
    expand = (jnp.arange(LANES)[:, None] == (jnp.arange(D_INNER)[None, :] // SSM_HEAD_DIM)).astype(BF16)
    mix_w = (w_ssm_br[0].astype(BF16), w_attn_br[0].astype(BF16), w_out[0].astype(BF16),
             norm_mlp[0][None, :], w_up[0].astype(BF16), w_down[0].astype(BF16), norm_final[None, :])
    sink = sinks[0]

    tm = 512
    xp = x_prompt.reshape(nb * seq, D_MODEL)
    tabs_p = _rope_tables(jnp.arange(seq, dtype=jnp.int32))
    z, xbc, qkv, g, dt, xtail, kvtail = _proj(xp, nw, proj_w, tabs_p, tm=tm, seq_len=seq,
                                              out_dtype=BF16, emit_tails=True)
    ys, ssm_p = _ssd_prompt(z, xbc, dt, ssd_params, _same_seq_causal(CHUNK, CHUNK), expand,
                            n_seq=nb, n_chunks=seq // CHUNK)
    ya = _attn_prompt(qkv, sink, n_seq=nb, n_blocks=seq // ATTN_BLOCK)
    y_prompt = _mix(xp, ys, ya, g, mix_w, tm=256).reshape(nb, seq, D_MODEL)
    conv_p = xtail[:, SUBLANES - (CONV_K - 1):, :]
    k_p = kvtail[:, :, :KV_WIDTH].reshape(nb, WINDOW, N_KV_HEADS, HEAD_DIM)
    v_p = kvtail[:, :, KV_WIDTH:].reshape(nb, WINDOW, N_KV_HEADS, HEAD_DIM)

    xs = x_sample.reshape(db * dseq, D_MODEL)
    tm_s = 256
    pos_s = PAST_LEN + (jnp.arange(tm_s, dtype=jnp.int32) % dseq)
    z, xbc, qkv, g, dt = _proj(xs, nw, proj_w, _rope_tables(pos_s), tm=tm_s, seq_len=dseq,
                               out_dtype=F32, emit_tails=False)
    seqs_per_block = CHUNK // dseq
    cst = jnp.pad(state_conv[0], ((0, 0), (dseq - (CONV_K - 1), 0), (0, 0))).reshape(db * dseq, CONV_DIM)
    ys, ssm_s = _ssd_sample(z, xbc, dt, cst, state_ssm[0].reshape(db, D_INNER, D_STATE), ssd_params,
                            _same_seq_causal(CHUNK, dseq), expand, seq_len=dseq, seqs_per_block=seqs_per_block)
    ya, k_s, v_s = _attn_sample(qkv, cache_k_win[0].reshape(db, WINDOW, KV_WIDTH),
                                cache_v_win[0].reshape(db, WINDOW, KV_WIDTH), sink, seq_len=dseq)
    y_sample = _mix(xs, ys, ya, g, mix_w, tm=256).reshape(db, dseq, D_MODEL)
    conv_s = xbc.reshape(db, dseq, CONV_DIM)[:, dseq - (CONV_K - 1):, :]

    return (y_prompt, y_sample,
            ssm_p.reshape(1, nb, SSM_HEADS, SSM_HEAD_DIM, D_STATE),
            ssm_s.reshape(1, db, SSM_HEADS, SSM_HEAD_DIM, D_STATE),
            conv_p[None], conv_s[None],
            k_p[None], k_s.reshape(1, db, WINDOW, N_KV_HEADS, HEAD_DIM),
            v_p[None], v_s.reshape(1, db, WINDOW, N_KV_HEADS, HEAD_DIM))
```

```python
import functools

import jax
import jax.numpy as jnp
from jax import lax
from jax.experimental import pallas as pl
from jax.experimental.pallas import tpu as pltpu

D_MODEL = 1024
D_INNER = 2048
SSM_HEAD_DIM = 64
SSM_HEADS = 32
SSM_GROUPS = 8
HEADS_PER_GROUP = 4
GROUP_WIDTH = HEADS_PER_GROUP * SSM_HEAD_DIM
D_STATE = 128
CONV_K = 4
CONV_DIM = 4096
B_OFFSET = D_INNER
C_OFFSET = D_INNER + SSM_GROUPS * D_STATE
CHUNK = 128
N_Q_HEADS = 16
N_KV_HEADS = 4
HEAD_DIM = 64
Q_PER_KV = 4
WINDOW = 128
ATTN_BLOCK = 128
ROT_DIM = 16
ROPE_THETA = 500000.0
D_FF = 4096
EPS = 1e-6
PAST_LEN = 16384
Q_WIDTH = N_Q_HEADS * HEAD_DIM
KV_WIDTH = N_KV_HEADS * HEAD_DIM
QKV_WIDTH = Q_WIDTH + 2 * KV_WIDTH

LANES = 128
SUBLANES = 8
VMEM_LIMIT = 56 * 1024 * 1024

F32 = jnp.float32
BF16 = jnp.bfloat16


def _resident(shape):
    zeros = (0,) * len(shape)
    return pl.BlockSpec(shape, lambda *_: zeros, pipeline_mode=pl.Buffered(1))


def _split(a, pieces):
    out = []
    for _ in range(pieces - 1):
        p = a.astype(BF16)
        out.append(p)
        a = a - p.astype(F32)
    out.append(a.astype(BF16))
    return out


def _dot(a, b):
    return jnp.dot(a, b, preferred_element_type=F32)


def _dot_nt(a, b):
    return lax.dot_general(a, b, (((1,), (1,)), ((), ())), preferred_element_type=F32)


def _silu(x):
    return x * (1.0 / (1.0 + jnp.exp(-x)))


def _sigmoid(x):
    return 1.0 / (1.0 + jnp.exp(-x))


PROJ_COLS = 512


def _rope_tile(x, cos, s1, s2):
    return x * cos + pltpu.roll(x, LANES - ROT_DIM // 2, 1) * s1 + pltpu.roll(x, ROT_DIM // 2, 1) * s2


def _proj_kernel(*refs, tiles_per_seq, seq_len, carry_conv):
    (x_ref, nw_ref, wz_ref, wx_ref, wqkv_ref, wg_ref, wdt_ref, cos_ref, s1_ref, s2_ref, cw_ref, cb_ref) = refs[:12]
    if carry_conv:
        sz_ref, xc_ref, qkv_ref, g_ref, dt_ref, xtail_ref, kvtail_ref, tail_ref = refs[12:]
    else:
        cst_ref, sz_ref, xc_ref, qkv_ref, g_ref, dt_ref, xraw_ref = refs[12:]
    tm = x_ref.shape[0]
    x = x_ref[...]
    u = (x * lax.rsqrt(jnp.mean(x * x, axis=-1, keepdims=True) + EPS) * nw_ref[...]).astype(BF16)
    step = PROJ_COLS
    if carry_conv:
        pos = pl.program_id(0) % tiles_per_seq
        is_first, is_last = pos == 0, pos == tiles_per_seq - 1
        row8 = lax.broadcasted_iota(jnp.int32, (SUBLANES, step), 0)

        @pl.when(is_first)
        def _():
            tail_ref[...] = jnp.zeros_like(tail_ref)
    else:
        tok = lax.broadcasted_iota(jnp.int32, (tm, step), 0) % seq_len

    for c in range(0, D_INNER, step):
        sz_ref[:, c:c + step] = _silu(_dot(u, wz_ref[:, c:c + step])).astype(sz_ref.dtype)

    for c in range(0, CONV_DIM, step):
        cols = slice(c, c + step)
        r = _dot(u, wx_ref[:, cols])
        cw = cw_ref[:, cols]
        acc = r * cw[CONV_K - 1:CONV_K, :] + cb_ref[:, cols]
        if carry_conv:
            tl = tail_ref[:, cols]
            tail_ref[:, cols] = r[tm - SUBLANES:, :]

            @pl.when(is_last)
            def _(r=r, cols=cols):
                xtail_ref[0, :, cols] = r[tm - SUBLANES:, :]
        else:
            cs = cst_ref[:, cols]
            xraw_ref[:, cols] = r
        for k in range(1, CONV_K):
            rolled = pltpu.roll(r, k, 0)
            if carry_conv:
                top = jnp.where(row8 < k, pltpu.roll(tl, k, 0), rolled[:SUBLANES])
                shifted = jnp.concatenate([top, rolled[SUBLANES:]], axis=0)
            else:
                shifted = jnp.where(tok < k, pltpu.roll(cs, tm - seq_len + k, 0), rolled)
            acc = acc + shifted * cw[CONV_K - 1 - k:CONV_K - k, :]
        xc_ref[:, cols] = _silu(acc).astype(xc_ref.dtype)

    for c in range(0, 2 * D_MODEL, step):
        g_ref[:, c:c + step] = _dot(u, wg_ref[:, c:c + step]).astype(g_ref.dtype)
    dt_ref[...] = _dot(u, wdt_ref[...])

    cos, s1, s2 = cos_ref[...], s1_ref[...], s2_ref[...]
    for c in range(0, QKV_WIDTH, 2 * LANES):
        r = _dot(u, wqkv_ref[:, c:c + 2 * LANES])
        halves = [r[:, :LANES], r[:, LANES:]]
        if c < Q_WIDTH + KV_WIDTH:
            halves = [_rope_tile(h, cos, s1, s2) for h in halves]
        for j, h in enumerate(halves):
            cc = c + j * LANES
            qkv_ref[:, cc:cc + LANES] = h.astype(qkv_ref.dtype)
            if carry_conv and cc >= Q_WIDTH:
                @pl.when(is_last)
                def _(h=h, cc=cc):
                    kvtail_ref[0, :, cc - Q_WIDTH:cc - Q_WIDTH + LANES] = h[tm - WINDOW:, :]


def _proj(x2d, nw, wts, rope_tabs, conv_wb, conv_state_rows, *, tm, seq_len, out_dtype):
    m = x2d.shape[0]
    n_tiles = m // tm
    tab_tiles = rope_tabs[0].shape[0] // tm
    carry_conv = conv_state_rows is None
    tiles_per_seq = max(seq_len // tm, 1)
    wz, wx, wqkv, wg, wdt = wts
    cw, cb = conv_wb
    row = lambda w: pl.BlockSpec((tm, w), lambda i: (i, 0))
    tab = pl.BlockSpec((tm, LANES), lambda i: (i % tab_tiles, 0))
    in_specs = [row(D_MODEL), _resident((1, D_MODEL)), _resident(wz.shape), _resident(wx.shape),
                _resident(wqkv.shape), _resident(wg.shape), _resident(wdt.shape), tab, tab, tab,
                _resident(cw.shape), _resident(cb.shape)]
    args = [x2d, nw, wz, wx, wqkv, wg, wdt, *rope_tabs, cw, cb]
    out_shape = [jax.ShapeDtypeStruct((m, D_INNER), out_dtype),
                 jax.ShapeDtypeStruct((m, CONV_DIM), out_dtype),
                 jax.ShapeDtypeStruct((m, QKV_WIDTH), out_dtype),
                 jax.ShapeDtypeStruct((m, 2 * D_MODEL), out_dtype),
                 jax.ShapeDtypeStruct((m, LANES), F32)]
    out_specs = [row(D_INNER), row(CONV_DIM), row(QKV_WIDTH), row(2 * D_MODEL), row(LANES)]
    scratch = []
    if carry_conv:
        n_seq = m // seq_len
        out_shape += [jax.ShapeDtypeStruct((n_seq, SUBLANES, CONV_DIM), F32),
                      jax.ShapeDtypeStruct((n_seq, WINDOW, 2 * KV_WIDTH), F32)]
        out_specs += [pl.BlockSpec((1, SUBLANES, CONV_DIM), lambda i: (i // tiles_per_seq, 0, 0)),
                      pl.BlockSpec((1, WINDOW, 2 * KV_WIDTH), lambda i: (i // tiles_per_seq, 0, 0))]
        scratch = [pltpu.VMEM((SUBLANES, CONV_DIM), F32)]
    else:
        in_specs.append(row(CONV_DIM))
        args.append(conv_state_rows)
        out_shape.append(jax.ShapeDtypeStruct((m, CONV_DIM), F32))
        out_specs.append(row(CONV_DIM))
    return pl.pallas_call(
        functools.partial(_proj_kernel, tiles_per_seq=tiles_per_seq, seq_len=seq_len, carry_conv=carry_conv),
        grid=(n_tiles,), in_specs=in_specs, out_specs=out_specs, out_shape=out_shape, scratch_shapes=scratch,
        compiler_params=pltpu.CompilerParams(dimension_semantics=("arbitrary",),
                                             vmem_limit_bytes=VMEM_LIMIT),
        name="proj",
    )(*args)


EXPAND_PIECES = 2


def _softplus(x):
    return jnp.maximum(x, 0.0) + jnp.log1p(jnp.exp(-jnp.abs(x)))


def _expand(a, e):
    return sum(_dot(p, e) for p in _split(a, EXPAND_PIECES))


def _ssd_decay_terms(dt_ref, dtb_ref, alog_ref, tri_ref):
    dtv = _softplus(dt_ref[...] + dtb_ref[...])
    dta = dtv * (-jnp.exp(alog_ref[...]))
    tri = tri_ref[...]
    acum = sum(_dot(tri, p) for p in _split(dta, 3))
    return dtv, acum


def _ssd_groups(xc_ref, y_ref, dtv, acum, tri_ref):
    t = xc_ref.shape[0]
    allowed = tri_ref[...] > 0
    shift_t = (acum - jnp.log(dtv)).T
    lane_head = lax.broadcasted_iota(jnp.int32, (t, GROUP_WIDTH), 1) // SSM_HEAD_DIM
    for g in range(SSM_GROUPS):
        bg = xc_ref[:, B_OFFSET + g * D_STATE:B_OFFSET + (g + 1) * D_STATE].astype(BF16)
        cg = xc_ref[:, C_OFFSET + g * D_STATE:C_OFFSET + (g + 1) * D_STATE].astype(BF16)
        cb = _dot_nt(cg, bg)
        xg = xc_ref[:, g * GROUP_WIDTH:(g + 1) * GROUP_WIDTH].astype(BF16)
        ms, bd = [], []
        for k in range(HEADS_PER_GROUP):
            h = g * HEADS_PER_GROUP + k
            seg = acum[:, h:h + 1] - shift_t[h:h + 1, :]
            ms.append((cb * jnp.exp(jnp.where(allowed, seg, -jnp.inf))).astype(BF16))
            bd.append(jnp.where(lane_head == k, xg, jnp.zeros_like(xg)))
        y_ref[:, g * GROUP_WIDTH:(g + 1) * GROUP_WIDTH] = _dot(jnp.concatenate(ms, axis=1),
                                                               jnp.concatenate(bd, axis=0))


def _ssd_finish(y, xs, sz, dskip, nrm):
    y = (y + xs * dskip) * sz
    return y * lax.rsqrt(jnp.mean(y * y, axis=-1, keepdims=True) + EPS) * nrm


def _ssd_prompt_kernel(sz_ref, xc_ref, dt_ref, dtb_ref, alog_ref, dskip_ref, nrm_ref, tri_ref, e_ref,
                       y_out_ref, st_out_ref, y_ref, ht_ref):
    c = pl.program_id(1)
    t = xc_ref.shape[0]

    @pl.when(c == 0)
    def _():
        ht_ref[...] = jnp.zeros_like(ht_ref)

    dtv, acum = _ssd_decay_terms(dt_ref, dtb_ref, alog_ref, tri_ref)
    _ssd_groups(xc_ref, y_ref, dtv, acum, tri_ref)

    e = e_ref[...]
    ea_x = _expand(jnp.exp(acum), e)
    send_x = _expand(dtv * jnp.exp(acum[t - 1:t, :] - acum), e)

    for g in range(SSM_GROUPS):
        cols = slice(g * GROUP_WIDTH, (g + 1) * GROUP_WIDTH)
        bg = xc_ref[:, B_OFFSET + g * D_STATE:B_OFFSET + (g + 1) * D_STATE].astype(F32)
        cg = xc_ref[:, C_OFFSET + g * D_STATE:C_OFFSET + (g + 1) * D_STATE]
        xs = xc_ref[:, cols].astype(F32)
        ea = ea_x[:, cols]
        ht = ht_ref[g]
        y = y_ref[:, cols] + _dot(cg, ht.astype(BF16)) * ea
        xw = (xs * send_x[:, cols]).astype(BF16)
        ht_ref[g] = ht * ea[t - 1:t, :] + _dot(bg.T.astype(BF16), xw)
        y_out_ref[:, cols] = _ssd_finish(y, xs, sz_ref[:, cols].astype(F32), dskip_ref[:, cols],
                                         nrm_ref[:, cols]).astype(y_out_ref.dtype)

    @pl.when(c == pl.num_programs(1) - 1)
    def _():
        for g in range(SSM_GROUPS):
            st_out_ref[0, g * GROUP_WIDTH:(g + 1) * GROUP_WIDTH, :] = ht_ref[g].T


def _ssd_prompt(sz, xc, dt, params, tri, e, *, n_seq, n_chunks):
    m = sz.shape[0]
    row = lambda w: pl.BlockSpec((CHUNK, w), lambda b, c: (b * n_chunks + c, 0))
    in_specs = [row(D_INNER), row(CONV_DIM), row(LANES)] + [_resident(p.shape) for p in params] + \
               [_resident(tri.shape), _resident(e.shape)]
    return pl.pallas_call(
        _ssd_prompt_kernel,
        grid=(n_seq, n_chunks), in_specs=in_specs,
        out_specs=[row(D_INNER), pl.BlockSpec((1, D_INNER, D_STATE), lambda b, c: (b, 0, 0))],
        out_shape=[jax.ShapeDtypeStruct((m, D_INNER), BF16),
                   jax.ShapeDtypeStruct((n_seq, D_INNER, D_STATE), F32)],
        scratch_shapes=[pltpu.VMEM((CHUNK, D_INNER), F32),
                        pltpu.VMEM((SSM_GROUPS, D_STATE, GROUP_WIDTH), F32)],
        compiler_params=pltpu.CompilerParams(dimension_semantics=("arbitrary", "arbitrary"),
                                             vmem_limit_bytes=VMEM_LIMIT),
        name="ssd_prompt",
    )(sz, xc, dt, *params, tri, e)


def _ssd_sample_kernel(sz_ref, xc_ref, dt_ref, st_ref, dtb_ref, alog_ref, dskip_ref, nrm_ref, tri_ref, e_ref,
                       y_out_ref, st_out_ref, y_ref, ea_ref, xw_ref, *, seq_len):
    j = pl.program_id(1)
    t = xc_ref.shape[0]

    @pl.when(j == 0)
    def _():
        dtv, acum = _ssd_decay_terms(dt_ref, dtb_ref, alog_ref, tri_ref)
        _ssd_groups(xc_ref, y_ref, dtv, acum, tri_ref)
        tokr = lax.broadcasted_iota(jnp.int32, (t, LANES), 0) % seq_len
        last = jnp.where(tokr == seq_len - 1, acum, 0.0)
        for s in range(1, seq_len):
            ahead = pltpu.roll(acum, t - s, 0)
            last = last + jnp.where(tokr == seq_len - 1 - s, ahead, 0.0)
        e = e_ref[...]
        ea_ref[...] = _expand(jnp.exp(acum), e)
        xw_ref[...] = xc_ref[:, :D_INNER] * _expand(dtv * jnp.exp(last - acum), e)

    r0 = pl.multiple_of(j * seq_len, SUBLANES)
    rows = pl.ds(r0, seq_len)
    sub = lax.broadcasted_iota(jnp.int32, (SUBLANES, GROUP_WIDTH), 0)
    zeros_b = jnp.zeros((SUBLANES, D_STATE), F32)
    ones_b = jnp.ones((SUBLANES, D_STATE), F32)
    for g in range(SSM_GROUPS):
        cols = slice(g * GROUP_WIDTH, (g + 1) * GROUP_WIDTH)
        hg = st_ref[0, cols, :]
        bg = xc_ref[rows, B_OFFSET + g * D_STATE:B_OFFSET + (g + 1) * D_STATE]
        cg = xc_ref[rows, C_OFFSET + g * D_STATE:C_OFFSET + (g + 1) * D_STATE]
        ea = ea_ref[rows, cols]
        c16 = jnp.concatenate([cg, jnp.zeros_like(cg)], axis=0).astype(BF16)
        y_ref[rows, cols] = y_ref[rows, cols] + _dot_nt(c16, hg.astype(BF16))[:seq_len] * ea
        vh, vm, vl = (p.astype(F32) for p in _split(ea[seq_len - 1:seq_len, :], 3))
        bot = jnp.where(sub == 0, vh, jnp.where(sub == 1, vm, jnp.where(sub == 2, vl, 0.0)))
        lhs = jnp.concatenate([xw_ref[rows, cols], bot], axis=0).astype(BF16)
        rhs = jnp.concatenate([jnp.concatenate([bg, zeros_b], axis=1),
                               jnp.concatenate([zeros_b, ones_b], axis=1)], axis=0).astype(BF16)
        out = lax.dot_general(lhs, rhs, (((0,), (0,)), ((), ())), preferred_element_type=F32)
        st_out_ref[0, cols, :] = out[:, D_STATE:] * hg + out[:, :D_STATE]

    @pl.when(j == pl.num_programs(1) - 1)
    def _():
        for g in range(SSM_GROUPS):
            cols = slice(g * GROUP_WIDTH, (g + 1) * GROUP_WIDTH)
            y_out_ref[:, cols] = _ssd_finish(y_ref[:, cols], xc_ref[:, cols], sz_ref[:, cols], dskip_ref[:, cols],
                                             nrm_ref[:, cols]).astype(y_out_ref.dtype)


def _ssd_sample(sz, xc, dt, state, params, tri, e, *, seq_len, seqs_per_block):
    m = sz.shape[0]
    t = seq_len * seqs_per_block
    row = lambda w: pl.BlockSpec((t, w), lambda i, j: (i, 0))
    st = pl.BlockSpec((1, D_INNER, D_STATE), lambda i, j: (i * seqs_per_block + j, 0, 0))
    in_specs = [row(D_INNER), row(CONV_DIM), row(LANES), st] + [_resident(p.shape) for p in params] + \
               [_resident(tri.shape), _resident(e.shape)]
    return pl.pallas_call(
        functools.partial(_ssd_sample_kernel, seq_len=seq_len),
        grid=(m // t, seqs_per_block), in_specs=in_specs,
        out_specs=[row(D_INNER), st],
        out_shape=[jax.ShapeDtypeStruct((m, D_INNER), BF16),
                   jax.ShapeDtypeStruct(state.shape, F32)],
        scratch_shapes=[pltpu.VMEM((t, D_INNER), F32), pltpu.VMEM((t, D_INNER), F32),
                        pltpu.VMEM((t, D_INNER), F32)],
        compiler_params=pltpu.CompilerParams(dimension_semantics=("arbitrary", "arbitrary"),
                                             vmem_limit_bytes=VMEM_LIMIT),
        name="ssd_sample",
    )(sz, xc, dt, state, *params, tri, e)


def _softmax_sink_pv(s_parts, v_parts, sink_col):
    m = sink_col
    for s in s_parts:
        m = jnp.maximum(m, jnp.max(s, axis=-1, keepdims=True))
    den = jnp.exp(sink_col - m)
    o = None
    for s, v in zip(s_parts, v_parts):
        p = jnp.exp(s - m)
        den = den + jnp.sum(p, axis=-1, keepdims=True)
        pv = _dot(p.astype(BF16), v)
        o = pv if o is None else o + pv
    return o * (1.0 / den)


def _sink_column(sinks_ref, kv, rows_per_head):
    rows = lax.broadcasted_iota(jnp.int32, (Q_PER_KV * rows_per_head, 1), 0) // rows_per_head
    col = jnp.zeros((Q_PER_KV * rows_per_head, 1), F32)
    for r in range(Q_PER_KV):
        col = jnp.where(rows == r, sinks_ref[kv * Q_PER_KV + r], col)
    return col


def _attn_prompt_kernel(sinks_ref, q_ref, kp_ref, kc_ref, vp_ref, vc_ref, o_ref):
    n = pl.program_id(1)
    blk = q_ref.shape[0]
    r = lax.broadcasted_iota(jnp.int32, (Q_PER_KV * blk, 2 * blk), 0) % blk
    c = lax.broadcasted_iota(jnp.int32, (Q_PER_KV * blk, 2 * blk), 1)
    ok = (c >= r) & (c <= r + WINDOW) & ((c >= blk) | (n > 0))
    scale = HEAD_DIM ** -0.5
    for kv in range(N_KV_HEADS):
        hs = slice(kv * HEAD_DIM, (kv + 1) * HEAD_DIM)
        kk = jnp.concatenate([kp_ref[:, hs], kc_ref[:, hs]], axis=0)
        vv = jnp.concatenate([vp_ref[:, hs], vc_ref[:, hs]], axis=0)
        qs = jnp.concatenate([q_ref[:, (kv * Q_PER_KV + i) * HEAD_DIM:(kv * Q_PER_KV + i + 1) * HEAD_DIM]
                              for i in range(Q_PER_KV)], axis=0)
        s = jnp.where(ok, _dot_nt(qs, kk) * scale, -jnp.inf)
        o = _softmax_sink_pv([s], [vv], _sink_column(sinks_ref, kv, blk))
        for i in range(Q_PER_KV):
            h = kv * Q_PER_KV + i
            o_ref[:, h * HEAD_DIM:(h + 1) * HEAD_DIM] = o[i * blk:(i + 1) * blk].astype(o_ref.dtype)


def _attn_prompt(qkv, sinks, *, n_seq, n_blocks):
    m = qkv.shape[0]
    kcol, vcol = Q_WIDTH // KV_WIDTH, Q_WIDTH // KV_WIDTH + 1
    cur = lambda b, n: b * n_blocks + n
    prev = lambda b, n: b * n_blocks + jnp.maximum(n - 1, 0)
    in_specs = [pl.BlockSpec(memory_space=pltpu.SMEM),
                pl.BlockSpec((ATTN_BLOCK, Q_WIDTH), lambda b, n: (cur(b, n), 0)),
                pl.BlockSpec((ATTN_BLOCK, KV_WIDTH), lambda b, n: (prev(b, n), kcol)),
                pl.BlockSpec((ATTN_BLOCK, KV_WIDTH), lambda b, n: (cur(b, n), kcol)),
                pl.BlockSpec((ATTN_BLOCK, KV_WIDTH), lambda b, n: (prev(b, n), vcol)),
                pl.BlockSpec((ATTN_BLOCK, KV_WIDTH), lambda b, n: (cur(b, n), vcol))]
    return pl.pallas_call(
        _attn_prompt_kernel, grid=(n_seq, n_blocks), in_specs=in_specs,
        out_specs=pl.BlockSpec((ATTN_BLOCK, Q_WIDTH), lambda b, n: (cur(b, n), 0)),
        out_shape=jax.ShapeDtypeStruct((m, Q_WIDTH), BF16),
        compiler_params=pltpu.CompilerParams(dimension_semantics=("arbitrary", "arbitrary"),
                                             vmem_limit_bytes=VMEM_LIMIT),
        name="attn_prompt",
    )(sinks, qkv, qkv, qkv, qkv, qkv)


def _attn_sample_kernel(sinks_ref, q_ref, kn_ref, vn_ref, ck_ref, cv_ref, o_ref, ko_ref, vo_ref):
    sl = q_ref.shape[0]
    ck, cv = ck_ref[0], cv_ref[0]
    kn, vn = kn_ref[...], vn_ref[...]
    ko_ref[0, :WINDOW - sl] = ck[sl:]
    ko_ref[0, WINDOW - sl:] = kn
    vo_ref[0, :WINDOW - sl] = cv[sl:]
    vo_ref[0, WINDOW - sl:] = vn
    rows = Q_PER_KV * sl
    tq_c = lax.broadcasted_iota(jnp.int32, (rows, WINDOW), 0) % sl
    c_c = lax.broadcasted_iota(jnp.int32, (rows, WINDOW), 1)
    ok_cache = c_c >= tq_c
    tq_n = lax.broadcasted_iota(jnp.int32, (rows, 2 * sl), 0) % sl
    c_n = lax.broadcasted_iota(jnp.int32, (rows, 2 * sl), 1)
    ok_new = c_n <= tq_n
    scale = HEAD_DIM ** -0.5
    pad = jnp.zeros((sl, KV_WIDTH), F32)
    kn16 = jnp.concatenate([kn, pad], axis=0).astype(BF16)
    vn16 = jnp.concatenate([vn, pad], axis=0).astype(BF16)
    ckb, cvb = ck.astype(BF16), cv.astype(BF16)
    for kv in range(N_KV_HEADS):
        hs = slice(kv * HEAD_DIM, (kv + 1) * HEAD_DIM)
        qs = jnp.concatenate([q_ref[:, (kv * Q_PER_KV + i) * HEAD_DIM:(kv * Q_PER_KV + i + 1) * HEAD_DIM]
                              for i in range(Q_PER_KV)], axis=0).astype(BF16)
        s_c = jnp.where(ok_cache, _dot_nt(qs, ckb[:, hs]) * scale, -jnp.inf)
        s_n = jnp.where(ok_new, _dot_nt(qs, kn16[:, hs]) * scale, -jnp.inf)
        o = _softmax_sink_pv([s_c, s_n], [cvb[:, hs], vn16[:, hs]], _sink_column(sinks_ref, kv, sl))
        for i in range(Q_PER_KV):
            h = kv * Q_PER_KV + i
            o_ref[:, h * HEAD_DIM:(h + 1) * HEAD_DIM] = o[i * sl:(i + 1) * sl].astype(o_ref.dtype)


def _attn_sample(qkv, cache_k, cache_v, sinks, *, seq_len):
    m = qkv.shape[0]
    n_seq = m // seq_len
    kcol, vcol = Q_WIDTH // KV_WIDTH, Q_WIDTH // KV_WIDTH + 1
    cache = pl.BlockSpec((1, WINDOW, KV_WIDTH), lambda b: (b, 0, 0))
    in_specs = [pl.BlockSpec(memory_space=pltpu.SMEM),
                pl.BlockSpec((seq_len, Q_WIDTH), lambda b: (b, 0)),
                pl.BlockSpec((seq_len, KV_WIDTH), lambda b: (b, kcol)),
                pl.BlockSpec((seq_len, KV_WIDTH), lambda b: (b, vcol)),
                cache, cache]
    return pl.pallas_call(
        _attn_sample_kernel, grid=(n_seq,), in_specs=in_specs,
        out_specs=[pl.BlockSpec((seq_len, Q_WIDTH), lambda b: (b, 0)), cache, cache],
        out_shape=[jax.ShapeDtypeStruct((m, Q_WIDTH), BF16),
                   jax.ShapeDtypeStruct(cache_k.shape, F32), jax.ShapeDtypeStruct(cache_v.shape, F32)],
        compiler_params=pltpu.CompilerParams(dimension_semantics=("arbitrary",),
                                             vmem_limit_bytes=VMEM_LIMIT),
        name="attn_sample",
    )(sinks, qkv, qkv, qkv, cache_k, cache_v)


FF_COLS = 1024


def _mix_kernel(x_ref, ys_ref, ya_ref, g_ref, wsb_ref, wab_ref, wo_ref, nm_ref, wu_ref, wd_ref, nf_ref, o_ref):
    g = g_ref[...].astype(F32)
    merged = (_sigmoid(g[:, :D_MODEL]) * _dot(ys_ref[...], wsb_ref[...])
              + _sigmoid(g[:, D_MODEL:]) * _dot(ya_ref[...], wab_ref[...]))
    x1 = x_ref[...] + _dot(merged.astype(BF16), wo_ref[...])
    hn = (x1 * lax.rsqrt(jnp.mean(x1 * x1, axis=-1, keepdims=True) + EPS) * nm_ref[...]).astype(BF16)
    x2 = x1
    for c in range(0, D_FF, FF_COLS):
        h = jnp.maximum(_dot(hn, wu_ref[:, c:c + FF_COLS]), 0.0)
        x2 = x2 + _dot((h * h).astype(BF16), wd_ref[c:c + FF_COLS, :])
    o_ref[...] = x2 * lax.rsqrt(jnp.mean(x2 * x2, axis=-1, keepdims=True) + EPS) * nf_ref[...]


def _mix(x2d, ys, ya, g, wts, *, tm):
    m = x2d.shape[0]
    row = lambda w: pl.BlockSpec((tm, w), lambda i: (i, 0))
    in_specs = [row(D_MODEL), row(D_INNER), row(Q_WIDTH), row(2 * D_MODEL)] + [_resident(w.shape) for w in wts]
    return pl.pallas_call(
        _mix_kernel, grid=(m // tm,), in_specs=in_specs, out_specs=row(D_MODEL),
        out_shape=jax.ShapeDtypeStruct((m, D_MODEL), F32),
        compiler_params=pltpu.CompilerParams(dimension_semantics=("arbitrary",),
                                             vmem_limit_bytes=VMEM_LIMIT),
        name="mix",
    )(x2d, ys, ya, g, *wts)


def _rope_tables(pos):
    half = ROT_DIM // 2
    inv = ROPE_THETA ** (-jnp.arange(half, dtype=F32) * 2.0 / ROT_DIM)
    ang = pos.astype(F32)[:, None] * inv[None, :]
    cos, sin = jnp.cos(ang), jnp.sin(ang)
    n = pos.shape[0]
    ones = jnp.ones((n, HEAD_DIM - ROT_DIM), F32)
    zeros = jnp.zeros((n, HEAD_DIM - ROT_DIM), F32)
    zh = jnp.zeros((n, half), F32)
    c = jnp.concatenate([cos, cos, ones], axis=1)
    s1 = jnp.concatenate([-sin, zh, zeros], axis=1)
    s2 = jnp.concatenate([zh, sin, zeros], axis=1)
    tile2 = lambda a: jnp.concatenate([a, a], axis=1)
    return tile2(c), tile2(s1), tile2(s2)


def _same_seq_causal(t, seq_len):
    r = jnp.arange(t)[:, None]
    c = jnp.arange(t)[None, :]
    return ((r // seq_len == c // seq_len) & (c <= r)).astype(BF16)


def kernel(x_prompt, x_sample, state_ssm, state_conv, cache_k_win, cache_v_win, norm_mix, w_in, conv_w,
           conv_b, dt_bias, a_log, d_skip, ssm_norm, sinks, w_ssm_br, w_attn_br, w_out, norm_mlp, w_up,
           w_down, norm_final):
    nb, seq, _ = x_prompt.shape
    db, dseq, _ = x_sample.shape
    assert norm_mix.shape[0] == 1, "single-layer step"
    assert seq % CHUNK == 0 and dseq == SUBLANES and dseq < CHUNK

    w = w_in[0]
    o_x, o_dt = D_INNER, D_INNER + CONV_DIM
    o_qkv = o_dt + SSM_HEADS
    o_g = o_qkv + QKV_WIDTH
    wz = w[:, :o_x].astype(BF16)
    wx = w[:, o_x:o_dt].astype(BF16)
    wdt = jnp.pad(w[:, o_dt:o_qkv], ((0, 0), (0, LANES - SSM_HEADS))).astype(BF16)
    wqkv = w[:, o_qkv:o_g].astype(BF16)
    wg = w[:, o_g:].astype(BF16)
    proj_w = (wz, wx, wqkv, wg, wdt)
    nw = norm_mix[0][None, :]
    conv_wb = (conv_w[0], conv_b[0][None, :])
    pad_heads = lambda a: jnp.pad(a[0], (0, LANES - SSM_HEADS))[None, :]
    ssd_params = (pad_heads(dt_bias), pad_heads(a_log),
                  jnp.repeat(d_skip[0], SSM_HEAD_DIM)[None, :], ssm_norm[0][None, :])
    expand = (jnp.arange(LANES)[:, None] == (jnp.arange(D_INNER)[None, :] // SSM_HEAD_DIM)).astype(BF16)
    mix_w = (w_ssm_br[0].astype(BF16), w_attn_br[0].astype(BF16), w_out[0].astype(BF16),
             norm_mlp[0][None, :], w_up[0].astype(BF16), w_down[0].astype(BF16), norm_final[None, :])
    sink = sinks[0]

    tm = 512
    xp = x_prompt.reshape(nb * seq, D_MODEL)
    tabs_p = _rope_tables(jnp.arange(seq, dtype=jnp.int32))
    sz, xc, qkv, g, dt, xtail, kvtail = _proj(xp, nw, proj_w, tabs_p, conv_wb, None, tm=tm, seq_len=seq,
                                              out_dtype=BF16)
    ys, ssm_p = _ssd_prompt(sz, xc, dt, ssd_params, _same_seq_causal(CHUNK, CHUNK), expand,
                            n_seq=nb, n_chunks=seq // CHUNK)
    ya = _attn_prompt(qkv, sink, n_seq=nb, n_blocks=seq // ATTN_BLOCK)
    y_prompt = _mix(xp, ys, ya, g, mix_w, tm=256).reshape(nb, seq, D_MODEL)
    conv_p = xtail[:, SUBLANES - (CONV_K - 1):, :]
    k_p = kvtail[:, :, :KV_WIDTH].reshape(nb, WINDOW, N_KV_HEADS, HEAD_DIM)
    v_p = kvtail[:, :, KV_WIDTH:].reshape(nb, WINDOW, N_KV_HEADS, HEAD_DIM)

    tm_s = CHUNK
    xs = x_sample.reshape(db * dseq, D_MODEL)
    pos_s = PAST_LEN + (jnp.arange(tm_s, dtype=jnp.int32) % dseq)
    cst = jnp.pad(state_conv[0], ((0, 0), (dseq - (CONV_K - 1), 0), (0, 0))).reshape(db * dseq, CONV_DIM)
    sz, xc, qkv, g, dt, xraw = _proj(xs, nw, proj_w, _rope_tables(pos_s), conv_wb, cst, tm=tm_s, seq_len=dseq,
                                     out_dtype=F32)
    ys, ssm_s = _ssd_sample(sz, xc, dt, state_ssm[0].reshape(db, D_INNER, D_STATE), ssd_params,
                            _same_seq_causal(CHUNK, dseq), expand, seq_len=dseq, seqs_per_block=CHUNK // dseq)
    ya, k_s, v_s = _attn_sample(qkv, cache_k_win[0].reshape(db, WINDOW, KV_WIDTH),
                                cache_v_win[0].reshape(db, WINDOW, KV_WIDTH), sink, seq_len=dseq)
    y_sample = _mix(xs, ys, ya, g, mix_w, tm=256).reshape(db, dseq, D_MODEL)
    conv_s = xraw.reshape(db, dseq, CONV_DIM)[:, dseq - (CONV_K - 1):, :]

    return (y_prompt, y_sample,
            ssm_p.reshape(1, nb, SSM_HEADS, SSM_HEAD_DIM, D_STATE),
            ssm_s.reshape(1, db, SSM_HEADS, SSM_HEAD_DIM, D_STATE),
            conv_p[None], conv_s[None],
            k_p[None], k_s.reshape(1, db, WINDOW, N_KV_HEADS, HEAD_DIM),
            v_p[None], v_s.reshape(1, db, WINDOW, N_KV_HEADS, HEAD_DIM))
```
